```python
import jax
import jax.numpy as jnp
from jax import lax
import numpy as np

D_MODEL = 2048
BATCH = 4
SEQ = 4096
DEPTH = 4

GRID_W = 64
CTX_LEN = 256
EPS = 1e-6
N_MOD = 6

MLA_HEADS = 8
MLA_Q_RANK = 512
MLA_KV_RANK = 256
MLA_NOPE = 128
MLA_ROPE = 64
MLA_V = 128
MLA_QK = MLA_NOPE + MLA_ROPE
ROPE_THETA = 10000.0
Q_BLOCK = 128

GDN_HEADS = 8
GDN_DK = 128
GDN_DV = 128
GDN_CONV = 5
GDN_CHUNK = 64

MLA_WIDTH = MLA_HEADS * MLA_V
GDN_WIDTH = GDN_HEADS * GDN_DV
MIX_WIDTH = MLA_WIDTH + GDN_WIDTH

P_Q_LAT = MLA_Q_RANK
P_KV_LAT = MLA_KV_RANK
P_K_PE = MLA_ROPE
P_GDN_QKV = GDN_HEADS * (2 * GDN_DK + GDN_DV)
P_GDN_Z = GDN_WIDTH
P_GDN_BETA = 2 * GDN_HEADS
P_GDN_ALPHA = 2 * GDN_HEADS
IN_WIDTH = P_Q_LAT + P_KV_LAT + P_K_PE + P_GDN_QKV + P_GDN_Z + P_GDN_BETA + P_GDN_ALPHA
SPLIT_POINTS = (P_Q_LAT,
                P_Q_LAT + P_KV_LAT,
                P_Q_LAT + P_KV_LAT + P_K_PE,
                P_Q_LAT + P_KV_LAT + P_K_PE + P_GDN_QKV,
                P_Q_LAT + P_KV_LAT + P_K_PE + P_GDN_QKV + P_GDN_Z,
                P_Q_LAT + P_KV_LAT + P_K_PE + P_GDN_QKV + P_GDN_Z + P_GDN_BETA)

N_EXPERTS = 64
N_GROUPS = 8
EXPERTS_PER_GROUP = N_EXPERTS // N_GROUPS
TOPK_GROUPS = 4
TOP_K = 8
EXPERT_FF = 256
SHARED_FF = 256
ROUTED_SCALE = 2.5
DISPATCH_BLOCK = 128

kernel_name = 'hybrid_mla_gdn_moe_dit'


def rmsnorm(x, g):
    xf = x.astype(jnp.float32)
    y = xf * lax.rsqrt(jnp.mean(xf * xf, axis=-1, keepdims=True) + EPS)
    return (y * g.astype(jnp.float32)).astype(x.dtype)


def l2norm(x):
    return x * lax.rsqrt(jnp.sum(x * x, axis=-1, keepdims=True) + EPS)


def modulate(h, shift, scale):
    return h * (1 + scale) + shift


def axial_rope_tables(rows):
    r = jnp.repeat(jnp.arange(rows, dtype=jnp.float32), GRID_W)
    col = jnp.tile(jnp.arange(GRID_W, dtype=jnp.float32), rows)
    half = MLA_ROPE // 2
    inv = jnp.power(ROPE_THETA, -jnp.arange(0, half, 2, dtype=jnp.float32) / half)
    ar = r[:, None] * inv
    ac = col[:, None] * inv
    ang = jnp.concatenate([ar, ar, ac, ac], axis=-1)
    return jnp.cos(ang), jnp.sin(ang)


def apply_axial_rope(x, cos, sin):
    a, b, cq, d = jnp.split(x, 4, axis=-1)
    rot = jnp.concatenate([-b, a, -d, cq], axis=-1)
    return (x * cos + rot * sin).astype(x.dtype)


def mla_q(q_lat, q_norm, w_q_up, rope):
    B, L, _ = q_lat.shape
    q = (rmsnorm(q_lat, q_norm) @ w_q_up).reshape(B, L, MLA_HEADS, MLA_QK)
    q_nope, q_pe = q[..., :MLA_NOPE], q[..., MLA_NOPE:]
    if rope is not None:
        cos, sin = rope
        q_pe = apply_axial_rope(q_pe, cos[:, None, :], sin[:, None, :])
    return jnp.concatenate([q_nope, q_pe], axis=-1)


def mla_kv(kv_lat, k_pe, kv_norm, w_kv_up, rope):
    B, L, _ = kv_lat.shape
    kv = (rmsnorm(kv_lat, kv_norm) @ w_kv_up).reshape(B, L, MLA_HEADS, MLA_NOPE + MLA_V)
    k_nope, v = kv[..., :MLA_NOPE], kv[..., MLA_NOPE:]
    if rope is not None:
        cos, sin = rope
        k_pe = apply_axial_rope(k_pe, cos, sin)
    k_pe = jnp.broadcast_to(k_pe[:, :, None, :], (B, L, MLA_HEADS, MLA_ROPE))
    return jnp.concatenate([k_nope, k_pe], axis=-1), v


def softmax_attend(q, k, v):
    s = jnp.einsum('bqhd,bkhd->bhqk', q, k, preferred_element_type=jnp.float32) * (MLA_QK ** -0.5)
    p = jax.nn.softmax(s, axis=-1).astype(v.dtype)
    return jnp.einsum('bhqk,bkhd->bqhd', p, v)


def latent_attention(q, k_all, v_all):
    B, S, H, Dq = q.shape
    nb = S // Q_BLOCK
    qb = q.reshape(B, nb, Q_BLOCK, H, Dq).swapaxes(0, 1)
    o = lax.map(lambda qi: softmax_attend(qi, k_all, v_all), qb)
    return o.swapaxes(0, 1).reshape(B, S, MLA_WIDTH)


def centred_conv(u, w):
    ch = u.shape[-1]
    pad = (GDN_CONV - 1) // 2
    return lax.conv_general_dilated(u, w[:, None, :].astype(u.dtype), window_strides=(1,),
                                    padding=[(pad, pad)], dimension_numbers=('NWC', 'WIO', 'NWC'),
                                    feature_group_count=ch)


def gdn_prepare(qkv, beta_raw, alpha_raw, conv_w, a_log, dt_bias):
    B, L, _ = qkv.shape
    u = jax.nn.silu(centred_conv(qkv, conv_w)).astype(jnp.float32)
    q, k, v = jnp.split(u, [GDN_HEADS * GDN_DK, 2 * GDN_HEADS * GDN_DK], axis=-1)
    q = l2norm(q.reshape(B, L, GDN_HEADS, GDN_DK)) * (GDN_DK ** -0.5)
    k = l2norm(k.reshape(B, L, GDN_HEADS, GDN_DK))
    v = v.reshape(B, L, GDN_HEADS, GDN_DV)
    beta = jax.nn.sigmoid(beta_raw.astype(jnp.float32)).reshape(B, L, 2, GDN_HEADS)
    g = -jnp.exp(a_log.astype(jnp.float32)) * jax.nn.softplus(
        alpha_raw.astype(jnp.float32).reshape(B, L, 2, GDN_HEADS) + dt_bias.astype(jnp.float32))
    return q, k, v, beta, g


def gated_delta_chunked(q, k, v, g, beta, state0):
    B, L, H, DK = q.shape
    DV = v.shape[-1]
    NC = L // GDN_CHUNK

    def chunks(t):
        return t.reshape(B, NC, GDN_CHUNK, H, -1).transpose(1, 0, 3, 2, 4)

    qc, kc, vc = chunks(q), chunks(k), chunks(v)
    gc = jnp.cumsum(g.reshape(B, NC, GDN_CHUNK, H).transpose(1, 0, 3, 2), axis=-1)
    bc = beta.reshape(B, NC, GDN_CHUNK, H).transpose(1, 0, 3, 2)
    incl = jnp.tril(jnp.ones((GDN_CHUNK, GDN_CHUNK), dtype=bool))
    strict = jnp.tril(jnp.ones((GDN_CHUNK, GDN_CHUNK), dtype=bool), -1)
    diff = gc[..., :, None] - gc[..., None, :]
    decay = jnp.where(incl, jnp.exp(jnp.where(incl, diff, 0.0)), 0.0)
    kb = kc * bc[..., None]
    a = jnp.where(strict, jnp.einsum('nbhid,nbhjd->nbhij', kb, kc) * decay, 0.0)
    t_sys = a + jnp.eye(GDN_CHUNK, dtype=a.dtype)
    u = lax.linalg.triangular_solve(t_sys, vc * bc[..., None], left_side=True, lower=True)
    w = lax.linalg.triangular_solve(t_sys, kb * jnp.exp(gc)[..., None], left_side=True, lower=True)
    qk = jnp.einsum('nbhid,nbhjd->nbhij', qc, kc) * decay

    def step(state, xs):
        q_i, k_i, u_i, w_i, qk_i, g_i = xs
        v_new = u_i - jnp.einsum('bhck,bhkv->bhcv', w_i, state)
        o_i = (jnp.einsum('bhck,bhkv->bhcv', q_i * jnp.exp(g_i)[..., None], state)
               + jnp.einsum('bhij,bhjv->bhiv', qk_i, v_new))
        g_last = g_i[..., -1:]
        k_dec = k_i * jnp.exp(g_last - g_i)[..., None]
        state = state * jnp.exp(g_last)[..., None] + jnp.einsum('bhck,bhcv->bhkv', k_dec, v_new)
        return state, o_i

    state, o = lax.scan(step, state0, (qc, kc, u, w, qk, gc))
    return state, o.transpose(1, 0, 3, 2, 4).reshape(B, L, H, DV)


def bidirectional_gdn(prep_ctx, prep_lat, need_ctx_out):
    qc, kc, vc, bc, gc = prep_ctx
    ql, kl, vl, bl, gl = prep_lat
    B = ql.shape[0]
    zero = jnp.zeros((B, GDN_HEADS, GDN_DK, GDN_DV), jnp.float32)

    def rev(t):
        return jnp.flip(t, axis=1)

    sc_f, oc_f = gated_delta_chunked(qc, kc, vc, gc[:, :, 0], bc[:, :, 0], zero)
    _, ol_f = gated_delta_chunked(ql, kl, vl, gl[:, :, 0], bl[:, :, 0], sc_f)
    sc_b, oc_b = gated_delta_chunked(rev(qc), rev(kc), rev(vc), rev(gc[:, :, 1]), rev(bc[:, :, 1]), zero)
    _, ol_b = gated_delta_chunked(rev(ql), rev(kl), rev(vl), rev(gl[:, :, 1]), rev(bl[:, :, 1]), sc_b)
    o_lat = ol_f + rev(ol_b)
    o_ctx = oc_f + rev(oc_b) if need_ctx_out else None
    return o_lat, o_ctx


def gdn_gated_norm(o, z, norm_w):
    B, L = o.shape[:2]
    y = rmsnorm(o, norm_w) * jax.nn.silu(z.astype(jnp.float32).reshape(B, L, GDN_HEADS, GDN_DV))
    return y.reshape(B, L, GDN_WIDTH)


def moe_ffn(h, router_w, router_bias, w_gate, w_up, w_down, ws_gate, ws_up, ws_down):
    T, D = h.shape
    scores = jax.nn.sigmoid(jnp.matmul(h, router_w, preferred_element_type=jnp.float32))
    sel = scores + router_bias.astype(jnp.float32)
    grp_score = lax.top_k(sel.reshape(T, N_GROUPS, EXPERTS_PER_GROUP), 2)[0].sum(-1)
    _, top_groups = lax.top_k(grp_score, TOPK_GROUPS)
    group_mask = jnp.any(top_groups[:, :, None] == jnp.arange(N_GROUPS)[None, None, :], axis=1)
    expert_mask = jnp.repeat(group_mask, EXPERTS_PER_GROUP, axis=1)
    _, idx = lax.top_k(jnp.where(expert_mask, sel, -jnp.inf), TOP_K)
    wts = jnp.take_along_axis(scores, idx, axis=1)
    wts = wts / jnp.sum(wts, axis=-1, keepdims=True) * ROUTED_SCALE

    n_pairs = T * TOP_K
    e_flat = idx.reshape(n_pairs)
    tok_flat = jnp.arange(n_pairs, dtype=jnp.int32) // TOP_K
    order = jnp.argsort(e_flat)
    e_sorted = e_flat[order]
    counts = jax.ops.segment_sum(jnp.ones((n_pairs,), jnp.int32), e_flat, num_segments=N_EXPERTS)
    padded = (counts + DISPATCH_BLOCK - 1) // DISPATCH_BLOCK * DISPATCH_BLOCK
    pad_end = jnp.cumsum(padded)
    pad_start = pad_end - padded
    cnt_start = jnp.cumsum(counts) - counts
    dest = pad_start[e_sorted] + jnp.arange(n_pairs, dtype=jnp.int32) - cnt_start[e_sorted]
    n_blocks = -(-n_pairs // DISPATCH_BLOCK) + N_EXPERTS
    n_slots = n_blocks * DISPATCH_BLOCK
    slot_tok = jnp.zeros((n_slots,), jnp.int32).at[dest].set(tok_flat[order])
    slot_w = jnp.zeros((n_slots,), jnp.float32).at[dest].set(wts.reshape(n_pairs)[order])
    block_start = jnp.arange(n_blocks, dtype=jnp.int32) * DISPATCH_BLOCK
    block_e = jnp.minimum(jnp.sum(pad_end[None, :] <= block_start[:, None], axis=1), N_EXPERTS - 1)

    def expert_block(y, blk):
        tok, wt, e = blk
        xb = h[tok]
        hid = jax.nn.silu(xb @ w_gate[e]) * (xb @ w_up[e])
        return y.at[tok].add((hid @ w_down[e]) * wt[:, None].astype(h.dtype)), None

    routed, _ = lax.scan(expert_block, jnp.zeros_like(h),
                         (slot_tok.reshape(n_blocks, DISPATCH_BLOCK), slot_w.reshape(n_blocks, DISPATCH_BLOCK), block_e))
    shared = (jax.nn.silu(h @ ws_gate) * (h @ ws_up)) @ ws_down
    return routed + shared


def setup_inputs(seed: int = 0) -> dict:
    key = jax.random.key(seed)
    ks = jax.random.split(key, 32)
    f32 = jnp.float32
    L = DEPTH
    D = D_MODEL

    def nrm(k, shape, fan_in, gain=1.0):
        return jax.random.normal(k, shape, f32) * (gain * fan_in ** -0.5)

    def gain_vec(k, shape):
        return 1.0 + 0.05 * jax.random.normal(k, shape, f32)

    dt = jax.random.uniform(ks[17], (L, 2, GDN_HEADS), f32, 0.001, 0.1)
    return {
        'x': jax.random.normal(ks[0], (BATCH, SEQ, D), f32),
        'c': jax.random.normal(ks[1], (BATCH, D), f32),
        'ctx': jax.random.normal(ks[2], (BATCH, CTX_LEN, D), f32),
        'c_ctx': jax.random.normal(ks[3], (D,), f32),
        'ada_w': nrm(ks[4], (L, D, N_MOD * D), D, 0.2),
        'ada_b': 0.02 * jax.random.normal(ks[5], (L, N_MOD * D), f32),
        'norm_mix_pre': gain_vec(ks[6], (L, D)),
        'norm_mix_post': gain_vec(ks[7], (L, D)),
        'norm_ffn_pre': gain_vec(ks[8], (L, D)),
        'norm_ffn_post': gain_vec(ks[9], (L, D)),
        'w_in': nrm(ks[10], (L, D, IN_WIDTH), D),
        'mla_q_norm': gain_vec(ks[11], (L, MLA_Q_RANK)),
        'mla_w_q_up': nrm(ks[12], (L, MLA_Q_RANK, MLA_HEADS * MLA_QK), MLA_Q_RANK),
        'mla_kv_norm': gain_vec(ks[13], (L, MLA_KV_RANK)),
        'mla_w_kv_up': nrm(ks[14], (L, MLA_KV_RANK, MLA_HEADS * (MLA_NOPE + MLA_V)), MLA_KV_RANK),
        'gdn_conv': nrm(ks[15], (L, GDN_CONV, P_GDN_QKV), GDN_CONV),
        'gdn_a_log': jnp.log(jax.random.uniform(ks[16], (L, 2, GDN_HEADS), f32, 1.0, 16.0)),
        'gdn_dt_bias': jnp.log(jnp.expm1(dt)),
        'gdn_norm': gain_vec(ks[18], (L, GDN_DV)),
        'w_out': nrm(ks[19], (L, MIX_WIDTH, D), MIX_WIDTH),
        'router_w': nrm(ks[20], (L, D, N_EXPERTS), D),
        'router_bias': 0.01 * jax.random.normal(ks[21], (L, N_EXPERTS), f32),
        'exp_w_gate': nrm(ks[22], (L, N_EXPERTS, D, EXPERT_FF), D),
        'exp_w_up': nrm(ks[23], (L, N_EXPERTS, D, EXPERT_FF), D),
        'exp_w_down': nrm(ks[24], (L, N_EXPERTS, EXPERT_FF, D), EXPERT_FF),
        'sh_w_gate': nrm(ks[25], (L, D, SHARED_FF), D),
        'sh_w_up': nrm(ks[26], (L, D, SHARED_FF), D),
        'sh_w_down': nrm(ks[27], (L, SHARED_FF, D), SHARED_FF),
    }


def reference(x, c, ctx, c_ctx, ada_w, ada_b, norm_mix_pre, norm_mix_post, norm_ffn_pre, norm_ffn_post,
              w_in, mla_q_norm, mla_w_q_up, mla_kv_norm, mla_w_kv_up, gdn_conv, gdn_a_log, gdn_dt_bias,
              gdn_norm, w_out, router_w, router_bias, exp_w_gate, exp_w_up, exp_w_down,
              sh_w_gate, sh_w_up, sh_w_down):
    B, S, D = x.shape
    C = ctx.shape[1]
    rows = S // GRID_W
    rope = axial_rope_tables(rows)
    xc = ctx
    for i in range(DEPTH):
        last = i == DEPTH - 1
        mod = jax.nn.silu(c) @ ada_w[i] + ada_b[i]
        mod_c = jax.nn.silu(c_ctx) @ ada_w[i] + ada_b[i]
        sh1, sc1, g1, sh2, sc2, g2 = [m[:, None, :] for m in jnp.split(mod, N_MOD, axis=-1)]
        sh1c, sc1c, g1c, sh2c, sc2c, g2c = jnp.split(mod_c, N_MOD, axis=-1)

        h = modulate(rmsnorm(x, norm_mix_pre[i]), sh1, sc1)
        hc = modulate(rmsnorm(xc, norm_mix_pre[i]), sh1c, sc1c)
        pl = jnp.split(h @ w_in[i], SPLIT_POINTS, axis=-1)
        pc = jnp.split(hc @ w_in[i], SPLIT_POINTS, axis=-1)

        k_c, v_c = mla_kv(pc[1], pc[2], mla_kv_norm[i], mla_w_kv_up[i], None)
        k_l, v_l = mla_kv(pl[1], pl[2], mla_kv_norm[i], mla_w_kv_up[i], rope)
        q_l = mla_q(pl[0], mla_q_norm[i], mla_w_q_up[i], rope)
        attn_l = latent_attention(q_l, jnp.concatenate([k_c, k_l], axis=1), jnp.concatenate([v_c, v_l], axis=1))

        prep_c = gdn_prepare(pc[3], pc[5], pc[6], gdn_conv[i], gdn_a_log[i], gdn_dt_bias[i])
        prep_l = gdn_prepare(pl[3], pl[5], pl[6], gdn_conv[i], gdn_a_log[i], gdn_dt_bias[i])
        o_l, o_c = bidirectional_gdn(prep_c, prep_l, not last)
        gdn_l = gdn_gated_norm(o_l, pl[4], gdn_norm[i]).astype(x.dtype)

        y = jnp.concatenate([attn_l, gdn_l], axis=-1) @ w_out[i]
        x = x + g1 * rmsnorm(y, norm_mix_post[i])
        if not last:
            q_c = mla_q(pc[0], mla_q_norm[i], mla_w_q_up[i], None)
            attn_c = softmax_attend(q_c, k_c, v_c).reshape(B, C, MLA_WIDTH)
            gdn_c = gdn_gated_norm(o_c, pc[4], gdn_norm[i]).astype(xc.dtype)
            yc = jnp.concatenate([attn_c, gdn_c], axis=-1) @ w_out[i]
            xc = xc + g1c * rmsnorm(yc, norm_mix_post[i])

        h2 = modulate(rmsnorm(x, norm_ffn_pre[i]), sh2, sc2).reshape(B * S, D)
        if not last:
            h2c = modulate(rmsnorm(xc, norm_ffn_pre[i]), sh2c, sc2c).reshape(B * C, D)
            tokens = jnp.concatenate([h2, h2c], axis=0)
        else:
            tokens = h2
        f = moe_ffn(tokens, router_w[i], router_bias[i], exp_w_gate[i], exp_w_up[i], exp_w_down[i],
                    sh_w_gate[i], sh_w_up[i], sh_w_down[i])
        x = x + g2 * rmsnorm(f[:B * S].reshape(B, S, D), norm_ffn_post[i])
        if not last:
            xc = xc + g2c * rmsnorm(f[B * S:].reshape(B, C, D), norm_ffn_post[i])
    return x
```

```python
import functools

import jax
import jax.numpy as jnp
from jax import lax
from jax.experimental import pallas as pl
from jax.experimental.pallas import tpu as pltpu

F32 = jnp.float32
BF16 = jnp.bfloat16
I32 = jnp.int32

EPS = 1e-6
N_MOD = 6
HEADS = 8
MLA_NOPE = 128
MLA_ROPE = 64
MLA_V = 128
MLA_QK = MLA_NOPE + MLA_ROPE
ROPE_THETA = 10000.0
GRID_W = 64
GDN_DK = 128
GDN_DV = 128
GDN_CONV = 5
GDN_CHUNK = 64
N_GROUPS = 8
TOPK_GROUPS = 4
TOP_K = 8
ROUTED_SCALE = 2.5
LANE = 128
VMEM_LIMIT = 56 * 1024 * 1024

NN = (((1,), (0,)), ((), ()))
NT = (((1,), (1,)), ((), ()))
TN = (((0,), (0,)), ((), ()))


def _cparams(sem):
    return pltpu.CompilerParams(dimension_semantics=sem, vmem_limit_bytes=VMEM_LIMIT)


def _tile(n, target, mult):
    best = None
    for t in range(mult, min(n, target) + 1, mult):
        if n % t == 0:
            best = t
    return best if best is not None else n


def _mm(a, b, dims=NN):
    return lax.dot_general(a.astype(BF16), b.astype(BF16), dims, preferred_element_type=F32)


def _mm3(a, b, dims=NN):
    a_hi = a.astype(BF16)
    b_hi = b.astype(BF16)
    a_lo = (a - a_hi.astype(F32)).astype(BF16)
    b_lo = (b - b_hi.astype(F32)).astype(BF16)
    dot = functools.partial(lax.dot_general, dimension_numbers=dims, preferred_element_type=F32)
    return dot(a_hi, b_hi) + dot(a_hi, b_lo) + dot(a_lo, b_hi)


def _rms(x):
    return x * lax.rsqrt(jnp.mean(x * x, axis=-1, keepdims=True) + EPS)


def _silu(x):
    return x * jax.nn.sigmoid(x)


def _is_ctx_rows(tile_idx, tm, n_lat):
    row = tile_idx * tm + lax.broadcasted_iota(I32, (tm, 1), 0)
    return row >= n_lat


def _ada_kernel(c_ref, w_ref, b_ref, o_ref):
    a = _silu(c_ref[...])
    o_ref[...] = _mm(a, w_ref[...]) + b_ref[...]


def _ada_mod(cs, ada_w, ada_b):
    depth, d, n = ada_w.shape
    rows = cs.shape[0]
    tn = _tile(n, 1024, LANE)
    return pl.pallas_call(
        _ada_kernel,
        grid=(depth, n // tn),
        in_specs=[pl.BlockSpec((rows, d), lambda l, j: (0, 0)),
                  pl.BlockSpec((None, d, tn), lambda l, j: (l, 0, j)),
                  pl.BlockSpec((None, 1, tn), lambda l, j: (l, 0, j))],
        out_specs=pl.BlockSpec((None, rows, tn), lambda l, j: (l, 0, j)),
        out_shape=jax.ShapeDtypeStruct((depth, rows, n), F32),
        compiler_params=_cparams(("parallel", "parallel")),
    )(cs, ada_w, ada_b.reshape(depth, 1, n))


def _nmm_kernel(*refs, tm, n_lat, modulate):
    if modulate:
        x_ref, g_ref, sh_ref, sc_ref, w_ref, o_ref, xn_ref = refs
    else:
        x_ref, g_ref, w_ref, o_ref, xn_ref = refs

    @pl.when(pl.program_id(2) == 0)
    def _():
        y = _rms(x_ref[...]) * g_ref[...]
        if modulate:
            ctx = _is_ctx_rows(pl.program_id(1), tm, n_lat)
            sh = jnp.where(ctx, sh_ref[1:2, :], sh_ref[0:1, :])
            sc = jnp.where(ctx, sc_ref[1:2, :], sc_ref[0:1, :])
            y = y * (1.0 + sc) + sh
        xn_ref[...] = y.astype(BF16)

    o_ref[...] = jnp.dot(xn_ref[...], w_ref[...], preferred_element_type=F32)


def _norm_matmul(x, col_block, k, gamma, w, n_lat, shift=None, scale=None):
    b, lt, _ = x.shape
    n = w.shape[1]
    tm = _tile(lt, 544, 16)
    tn = _tile(n, 1024, LANE)
    modulate = shift is not None
    in_specs = [pl.BlockSpec((None, tm, k), lambda bi, j, ni: (bi, j, col_block)),
                pl.BlockSpec((1, k), lambda bi, j, ni: (0, 0))]
    args = [x, gamma.reshape(1, k)]
    if modulate:
        in_specs += [pl.BlockSpec((None, 2, k), lambda bi, j, ni: (bi, 0, 0)),
                     pl.BlockSpec((None, 2, k), lambda bi, j, ni: (bi, 0, 0))]
        args += [shift, scale]
    in_specs.append(pl.BlockSpec((k, tn), lambda bi, j, ni: (0, ni)))
    args.append(w)
    return pl.pallas_call(
        functools.partial(_nmm_kernel, tm=tm, n_lat=n_lat, modulate=modulate),
        grid=(b, lt // tm, n // tn),
        in_specs=in_specs,
        out_specs=pl.BlockSpec((None, tm, tn), lambda bi, j, ni: (bi, j, ni)),
        out_shape=jax.ShapeDtypeStruct((b, lt, n), F32),
        scratch_shapes=[pltpu.VMEM((tm, k), BF16)],
        compiler_params=_cparams(("parallel", "parallel", "arbitrary")),
    )(*args)


def _attn_kernel(q_ref, k_ref, v_ref, o_ref, *, tq, n_lat, n_all):
    s = lax.dot_general(q_ref[...], k_ref[...], NT, preferred_element_type=F32) * (MLA_QK ** -0.5)
    first_key = jnp.where(pl.program_id(2) * tq >= n_lat, n_lat, 0)
    col = lax.broadcasted_iota(I32, (1, n_all), 1)
    s = jnp.where(col >= first_key, s, -jnp.inf)
    p = jnp.exp(s - jnp.max(s, axis=-1, keepdims=True))
    l = jnp.sum(p, axis=-1, keepdims=True)
    o_ref[...] = (jnp.dot(p.astype(BF16), v_ref[...], preferred_element_type=F32) / l).astype(o_ref.dtype)


def _attention(q, k, v, n_lat):
    b, h, lt, dq = q.shape
    tq = _tile(lt - n_lat, 256, 16)
    return pl.pallas_call(
        functools.partial(_attn_kernel, tq=tq, n_lat=n_lat, n_all=lt),
        grid=(b, h, lt // tq),
        in_specs=[pl.BlockSpec((None, None, tq, dq), lambda bi, hi, qi: (bi, hi, qi, 0)),
                  pl.BlockSpec((None, None, lt, dq), lambda bi, hi, qi: (bi, hi, 0, 0)),
                  pl.BlockSpec((None, None, lt, MLA_V), lambda bi, hi, qi: (bi, hi, 0, 0))],
        out_specs=pl.BlockSpec((None, tq, MLA_V), lambda bi, hi, qi: (bi, qi, hi)),
        out_shape=jax.ShapeDtypeStruct((b, lt, h * MLA_V), BF16),
        compiler_params=_cparams(("parallel", "parallel", "arbitrary")),
    )(q, k, v)


def _gdn_chunk_id(d, n, nc_lat, nc_all):
    fwd = jnp.where(n < nc_all - nc_lat, nc_lat + n, n - (nc_all - nc_lat))
    return jnp.where(d == 0, fwd, nc_all - 1 - n)


def _gdn_kernel(q_ref, k_ref, v_ref, beta_ref, gcol_ref, grow_ref, o_ref, st_ref):
    d = pl.program_id(1)
    ch = GDN_CHUNK

    @pl.when(pl.program_id(2) == 0)
    def _():
        st_ref[...] = jnp.zeros_like(st_ref)

    ii = lax.broadcasted_iota(I32, (ch, ch), 0)
    jj = lax.broadcasted_iota(I32, (ch, ch), 1)
    order = (ii - jj) * jnp.where(d == 0, 1, -1)
    incl = order >= 0
    strict = order > 0
    incl_f = incl.astype(F32)
    incl_t = (order <= 0).astype(F32)
    eye = (ii == jj).astype(F32)
    g_col = gcol_ref[...]
    g_row = grow_ref[...]
    beta = beta_ref[...]
    g_last = jnp.sum(g_col, axis=0, keepdims=True)

    heads = range(HEADS)
    sls = [slice(h * GDN_DK, (h + 1) * GDN_DK) for h in heads]
    q = [q_ref[:, sl] for sl in sls]
    k = [k_ref[:, sl] for sl in sls]
    v = [v_ref[:, sl] for sl in sls]
    b = [beta[:, h:h + 1] for h in heads]
    gc = [jnp.sum(incl_f * g_row[h:h + 1, :], axis=1, keepdims=True) for h in heads]
    gr = [jnp.sum(incl_t * g_col[:, h:h + 1], axis=0, keepdims=True) for h in heads]
    gl = [g_last[:, h:h + 1] for h in heads]
    decay = [jnp.where(incl, jnp.exp(jnp.where(incl, gc[h] - gr[h], 0.0)), 0.0) for h in heads]
    kb = [k[h] * b[h] for h in heads]
    p = [-jnp.where(strict, _mm(kb[h], k[h], NT) * decay[h], 0.0) for h in heads]
    t = [eye + p[h] for h in heads]
    for _ in range(ch.bit_length() - 2):
        p = [_mm3(p[h], p[h]) for h in heads]
        t = [t[h] + _mm3(t[h], p[h]) for h in heads]
    eg = [jnp.exp(gc[h]) for h in heads]
    uw = [_mm3(t[h], jnp.concatenate([v[h] * b[h], kb[h] * eg[h]], axis=1)) for h in heads]
    qk = [_mm(q[h], k[h], NT) * decay[h] for h in heads]
    s = [st_ref[h] for h in heads]
    v_new = [uw[h][:, :GDN_DV] - _mm(uw[h][:, GDN_DV:], s[h]) for h in heads]
    for h in heads:
        o_ref[:, sls[h]] = _mm(q[h] * eg[h], s[h]) + _mm(qk[h], v_new[h])
    for h in heads:
        k_dec = k[h] * jnp.exp(gl[h] - gc[h])
        st_ref[h] = s[h] * jnp.exp(gl[h]) + _mm(k_dec, v_new[h], TN)


def _gdn_scan(qkv, beta_col, g_col, g_row, n_lat):
    b, lt, width3 = qkv.shape
    width = width3 // 3
    ch = GDN_CHUNK
    nc_all = lt // ch
    nc_lat = n_lat // ch
    cid = functools.partial(_gdn_chunk_id, nc_lat=nc_lat, nc_all=nc_all)
    seq = lambda part: pl.BlockSpec((None, ch, width), lambda bi, d, n: (bi, cid(d, n), part))
    col = pl.BlockSpec((None, None, None, ch, HEADS), lambda bi, d, n: (bi, d, cid(d, n), 0, 0))
    row = pl.BlockSpec((None, None, None, HEADS, ch), lambda bi, d, n: (bi, d, cid(d, n), 0, 0))
    return pl.pallas_call(
        _gdn_kernel,
        grid=(b, 2, nc_all),
        in_specs=[seq(0), seq(1), seq(2), col, col, row],
        out_specs=pl.BlockSpec((None, None, ch, width), lambda bi, d, n: (bi, d, cid(d, n), 0)),
        out_shape=jax.ShapeDtypeStruct((b, 2, lt, width), F32),
        scratch_shapes=[pltpu.VMEM((HEADS, GDN_DK, GDN_DV), F32)],
        compiler_params=_cparams(("parallel", "parallel", "arbitrary")),
    )(qkv, qkv, qkv, beta_col, g_col, g_row)


def _outproj_kernel(attn_ref, of_ref, ob_ref, z_ref, gn_ref, w_ref, x_ref, gpost_ref, gate_ref,
                    gpre_ref, sh_ref, sc_ref, xo_ref, h_ref, *, tm, n_lat):
    ctx = _is_ctx_rows(pl.program_id(1), tm, n_lat)
    o = of_ref[...] + ob_ref[...]
    gated = []
    for h in range(HEADS):
        sl = slice(h * GDN_DV, (h + 1) * GDN_DV)
        gated.append((_rms(o[:, sl]) * gn_ref[...] * _silu(z_ref[:, sl])).astype(BF16))
    n_attn = attn_ref.shape[1]
    y = (jnp.dot(attn_ref[...], w_ref[:n_attn, :], preferred_element_type=F32)
         + jnp.dot(jnp.concatenate(gated, axis=1), w_ref[n_attn:, :], preferred_element_type=F32))
    gate = jnp.where(ctx, gate_ref[1:2, :], gate_ref[0:1, :])
    xn = x_ref[...] + gate * (_rms(y) * gpost_ref[...])
    xo_ref[...] = xn
    sh = jnp.where(ctx, sh_ref[1:2, :], sh_ref[0:1, :])
    sc = jnp.where(ctx, sc_ref[1:2, :], sc_ref[0:1, :])
    h_ref[...] = (_rms(xn) * gpre_ref[...]) * (1.0 + sc) + sh


def _out_proj(attn, o2, proj, z_block, g_gdn, w_out, x, g_post, gate, g_pre, shift, scale, n_lat):
    b, lt, d = x.shape
    wa = attn.shape[2]
    wg = o2.shape[3]
    tm = _tile(lt, 272, 16)
    row = lambda width: pl.BlockSpec((None, tm, width), lambda bi, j: (bi, j, 0))
    vec = lambda width: pl.BlockSpec((1, width), lambda bi, j: (0, 0))
    tab = pl.BlockSpec((None, 2, d), lambda bi, j: (bi, 0, 0))
    return pl.pallas_call(
        functools.partial(_outproj_kernel, tm=tm, n_lat=n_lat),
        grid=(b, lt // tm),
        in_specs=[row(wa),
                  pl.BlockSpec((None, None, tm, wg), lambda bi, j: (bi, 0, j, 0)),
                  pl.BlockSpec((None, None, tm, wg), lambda bi, j: (bi, 1, j, 0)),
                  pl.BlockSpec((None, tm, wg), lambda bi, j: (bi, j, z_block)),
                  vec(GDN_DV),
                  pl.BlockSpec((wa + wg, d), lambda bi, j: (0, 0)), row(d),
                  vec(d), tab, vec(d), tab, tab],
        out_specs=[row(d), row(d)],
        out_shape=[jax.ShapeDtypeStruct((b, lt, d), F32), jax.ShapeDtypeStruct((b, lt, d), F32)],
        compiler_params=_cparams(("parallel", "parallel")),
    )(attn, o2, o2, proj, g_gdn.reshape(1, GDN_DV), w_out, x, g_post.reshape(1, d), gate,
      g_pre.reshape(1, d), shift, scale)


def _gdn_prep_kernel(u_ref, w_ref, o_ref, *, n_lat):
    j = pl.program_id(1)
    u = u_ref[...]
    lt = u.shape[0]
    pos = lax.broadcasted_iota(I32, (lt, 1), 0)
    side = 2 * (pos - n_lat) + 1
    pad = (GDN_CONV - 1) // 2
    acc = u * w_ref[pad:pad + 1, :]
    for tap in range(GDN_CONV):
        off = tap - pad
        if off == 0:
            continue
        src = pos + off
        valid = (src >= 0) & (src < lt) & (side * (side + 2 * off) > 0)
        shifted = pltpu.roll(u, (-off) % lt, 0)
        acc = acc + jnp.where(valid, shifted, 0.0) * w_ref[tap:tap + 1, :]
    y = _silu(acc)
    norm = lax.rsqrt(jnp.sum(y * y, axis=-1, keepdims=True) + EPS)
    norm = jnp.where(j < 2 * HEADS, norm, 1.0) * jnp.where(j < HEADS, GDN_DK ** -0.5, 1.0)
    o_ref[...] = y * norm


def _gdn_prep(proj, first_block, conv_w, n_lat):
    b, lt, _ = proj.shape
    n_heads = 3 * HEADS
    return pl.pallas_call(
        functools.partial(_gdn_prep_kernel, n_lat=n_lat),
        grid=(b, n_heads),
        in_specs=[pl.BlockSpec((None, lt, GDN_DK), lambda bi, j: (bi, 0, first_block + j)),
                  pl.BlockSpec((GDN_CONV, GDN_DK), lambda bi, j: (0, j))],
        out_specs=pl.BlockSpec((None, lt, GDN_DK), lambda bi, j: (bi, 0, j)),
        out_shape=jax.ShapeDtypeStruct((b, lt, n_heads * GDN_DK), F32),
        compiler_params=_cparams(("parallel", "parallel")),
    )(proj, conv_w)


def _first_max(v, iota, size):
    m = jnp.max(v, axis=0, keepdims=True)
    i = jnp.min(jnp.where(v == m, iota, float(size)), axis=0, keepdims=True)
    return m, iota == i, i


def _router_kernel(h_ref, rw_ref, bias_ref, idx_ref, wts_ref, chosen_ref):
    n_exp = rw_ref.shape[0]
    tm = h_ref.shape[0]
    per = n_exp // N_GROUPS
    logits = _mm(rw_ref[...], h_ref[...], NT)
    scores = jax.nn.sigmoid(logits)
    sel = scores + bias_ref[...]
    neg = jnp.float32(-jnp.inf)
    iota_p = lax.broadcasted_iota(I32, (per, tm), 0).astype(F32)
    iota_g = lax.broadcasted_iota(I32, (N_GROUPS, tm), 0).astype(F32)
    iota_e = lax.broadcasted_iota(I32, (n_exp, tm), 0).astype(F32)
    grp = []
    for g in range(N_GROUPS):
        v = sel[g * per:(g + 1) * per, :]
        m1, pick, _ = _first_max(v, iota_p, per)
        m2 = jnp.max(jnp.where(pick, neg, v), axis=0, keepdims=True)
        grp.append(m1 + m2)
    gsc = jnp.concatenate(grp, axis=0)
    gmask = jnp.zeros((N_GROUPS, tm), F32)
    for _ in range(TOPK_GROUPS):
        _, pick, _ = _first_max(gsc, iota_g, N_GROUPS)
        gmask = jnp.where(pick, 1.0, gmask)
        gsc = jnp.where(pick, neg, gsc)
    emask = jnp.concatenate(
        [jnp.broadcast_to(gmask[g:g + 1, :], (per, tm)) for g in range(N_GROUPS)], axis=0)
    cand = jnp.where(emask > 0.0, sel, neg)
    chosen = jnp.zeros((n_exp, tm), F32)
    idxs, picked = [], []
    for _ in range(TOP_K):
        _, pick, i = _first_max(cand, iota_e, n_exp)
        chosen = jnp.where(pick, 1.0, chosen)
        cand = jnp.where(pick, neg, cand)
        idxs.append(i)
        picked.append(jnp.sum(jnp.where(pick, scores, 0.0), axis=0, keepdims=True))
    w = jnp.concatenate(picked, axis=0)
    idx_ref[...] = jnp.concatenate(idxs, axis=0).astype(I32)
    wts_ref[...] = w / jnp.sum(w, axis=0, keepdims=True) * ROUTED_SCALE
    chosen_ref[...] = chosen


def _router(h, rw_t, bias):
    t, d = h.shape
    n_exp = rw_t.shape[0]
    tm = _tile(t, 512, LANE)
    return pl.pallas_call(
        _router_kernel,
        grid=(t // tm,),
        in_specs=[pl.BlockSpec((tm, d), lambda i: (i, 0)),
                  pl.BlockSpec((n_exp, d), lambda i: (0, 0)),
                  pl.BlockSpec((n_exp, 1), lambda i: (0, 0))],
        out_specs=[pl.BlockSpec((TOP_K, tm), lambda i: (0, i)),
                   pl.BlockSpec((TOP_K, tm), lambda i: (0, i)),
                   pl.BlockSpec((n_exp, tm), lambda i: (0, i))],
        out_shape=[jax.ShapeDtypeStruct((TOP_K, t), I32),
                   jax.ShapeDtypeStruct((TOP_K, t), F32),
                   jax.ShapeDtypeStruct((n_exp, t), F32)],
        compiler_params=_cparams(("parallel",)),
    )(h, rw_t, bias.reshape(n_exp, 1))


def _expert_kernel(be_ref, nused_ref, tok_ref, tok_next_ref, h_hbm, wg_ref, wu_ref, wd_ref, o_ref,
                   xbuf, sem, *, tb):
    del be_ref
    i = pl.program_id(0)
    n_used = nused_ref[0]
    cur = i % 2

    def gather(tok, slot):
        def issue(r, c):
            pltpu.make_async_copy(h_hbm.at[pl.ds(tok[0, r], 1)], xbuf.at[slot, pl.ds(r, 1)],
                                  sem.at[slot]).start()
            return c

        lax.fori_loop(0, tb, issue, 0, unroll=8)

    @pl.when(jnp.logical_and(i == 0, n_used > 0))
    def _():
        gather(tok_ref, 0)

    @pl.when(i + 1 < n_used)
    def _():
        gather(tok_next_ref, 1 - cur)

    @pl.when(i >= n_used)
    def _():
        o_ref[...] = jnp.zeros_like(o_ref)

    @pl.when(i < n_used)
    def _():
        pltpu.make_async_copy(h_hbm.at[pl.ds(0, tb)], xbuf.at[cur], sem.at[cur]).wait()
        x = xbuf[cur].astype(BF16)
        g = jnp.dot(x, wg_ref[...], preferred_element_type=F32)
        u = jnp.dot(x, wu_ref[...], preferred_element_type=F32)
        hid = (_silu(g) * u).astype(BF16)
        o_ref[...] = jnp.dot(hid, wd_ref[...], preferred_element_type=F32)


def _experts(h, slot_tok, block_e, n_used, w_gate, w_up, w_down, tb):
    t, d = h.shape
    n_exp, _, ff = w_gate.shape
    n_blocks = block_e.shape[0]
    tok_blocks = slot_tok.reshape(n_blocks, 1, tb)
    grid_spec = pltpu.PrefetchScalarGridSpec(
        num_scalar_prefetch=2,
        grid=(n_blocks,),
        in_specs=[pl.BlockSpec((None, 1, tb), lambda i, be, nu: (i, 0, 0), memory_space=pltpu.SMEM),
                  pl.BlockSpec((None, 1, tb), lambda i, be, nu: (jnp.minimum(i + 1, n_blocks - 1), 0, 0),
                               memory_space=pltpu.SMEM),
                  pl.BlockSpec(memory_space=pl.ANY),
                  pl.BlockSpec((None, d, ff), lambda i, be, nu: (be[i], 0, 0)),
                  pl.BlockSpec((None, d, ff), lambda i, be, nu: (be[i], 0, 0)),
                  pl.BlockSpec((None, ff, d), lambda i, be, nu: (be[i], 0, 0))],
        out_specs=pl.BlockSpec((tb, d), lambda i, be, nu: (i, 0)),
        scratch_shapes=[pltpu.VMEM((2, tb, d), F32), pltpu.SemaphoreType.DMA((2,))],
    )
    return pl.pallas_call(
        functools.partial(_expert_kernel, tb=tb),
        grid_spec=grid_spec,
        out_shape=jax.ShapeDtypeStruct((n_blocks * tb, d), F32),
        compiler_params=_cparams(("arbitrary",)),
    )(block_e, n_used, tok_blocks, tok_blocks, h, w_gate, w_up, w_down)


def _combine_kernel(slots_ref, ys_hbm, wts_ref, h_ref, x_ref, wsg_ref, wsu_ref, wsd_ref,
                    gpost_ref, gate_ref, xo_ref, buf, sem, *, tm, n_lat):
    for kk in range(TOP_K):
        def issue(r, c, kk=kk):
            pltpu.make_async_copy(ys_hbm.at[pl.ds(slots_ref[kk, r], 1)],
                                  buf.at[kk, pl.ds(r, 1)], sem).start()
            return c

        lax.fori_loop(0, tm, issue, 0, unroll=8)
    hb = h_ref[...].astype(BF16)
    sg = jnp.dot(hb, wsg_ref[...], preferred_element_type=F32)
    su = jnp.dot(hb, wsu_ref[...], preferred_element_type=F32)
    f = jnp.dot((_silu(sg) * su).astype(BF16), wsd_ref[...], preferred_element_type=F32)
    w = wts_ref[...]
    for kk in range(TOP_K):
        pltpu.make_async_copy(ys_hbm.at[pl.ds(0, tm)], buf.at[kk], sem).wait()
    for kk in range(TOP_K):
        f = f + w[:, kk:kk + 1] * buf[kk]
    ctx = _is_ctx_rows(pl.program_id(1), tm, n_lat)
    gate = jnp.where(ctx, gate_ref[1:2, :], gate_ref[0:1, :])
    xo_ref[...] = x_ref[...] + gate * (_rms(f) * gpost_ref[...])


def _combine(tok_slots, ys, wts, h, x, ws_gate, ws_up, ws_down, g_post, gate, n_lat):
    b, lt, d = x.shape
    ff = ws_gate.shape[1]
    tm = _tile(lt, 128, 16)
    nt = lt // tm
    row = lambda width: pl.BlockSpec((None, tm, width), lambda bi, j: (bi, j, 0))
    full = lambda r, c: pl.BlockSpec((r, c), lambda bi, j: (0, 0))
    return pl.pallas_call(
        functools.partial(_combine_kernel, tm=tm, n_lat=n_lat),
        grid=(b, nt),
        in_specs=[pl.BlockSpec((None, None, TOP_K, tm), lambda bi, j: (bi, j, 0, 0),
                               memory_space=pltpu.SMEM),
                  pl.BlockSpec(memory_space=pl.ANY),
                  row(TOP_K), row(d), row(d),
                  full(d, ff), full(d, ff), full(ff, d), full(1, d),
                  pl.BlockSpec((None, 2, d), lambda bi, j: (bi, 0, 0))],
        out_specs=row(d),
        out_shape=jax.ShapeDtypeStruct((b, lt, d), F32),
        scratch_shapes=[pltpu.VMEM((TOP_K, tm, d), F32), pltpu.SemaphoreType.DMA(())],
        compiler_params=_cparams(("arbitrary", "arbitrary")),
    )(tok_slots.reshape(b, nt, tm, TOP_K).transpose(0, 1, 3, 2), ys, wts,
      h, x, ws_gate, ws_up, ws_down, g_post.reshape(1, d), gate)


def _dispatch(idx_t, chosen_t, tb, n_blocks):
    n_exp, t = chosen_t.shape
    ch = chosen_t.astype(I32)
    cs = jnp.cumsum(ch, axis=1)
    counts = cs[:, -1]
    padded = (counts + tb - 1) // tb * tb
    pad_end = jnp.cumsum(padded)
    dest_full = (pad_end - padded)[:, None] + (cs - ch)
    dest = jnp.take_along_axis(dest_full, idx_t, axis=0)
    tok = jnp.broadcast_to(jnp.arange(t, dtype=I32)[None, :], dest.shape)
    slot_tok = jnp.zeros((n_blocks * tb,), I32).at[dest.reshape(-1)].set(tok.reshape(-1))
    block_start = jnp.arange(n_blocks, dtype=I32) * tb
    block_e = jnp.minimum(jnp.sum(pad_end[None, :] <= block_start[:, None], axis=1),
                          n_exp - 1).astype(I32)
    n_used = (pad_end[-1:] // tb).astype(I32)
    return slot_tok, block_e, n_used, dest.T


def _rope_tables(n_lat, n_ctx):
    rows = n_lat // GRID_W
    r = jnp.repeat(jnp.arange(rows, dtype=F32), GRID_W)
    col = jnp.tile(jnp.arange(GRID_W, dtype=F32), rows)
    half = MLA_ROPE // 2
    inv = jnp.power(ROPE_THETA, -jnp.arange(0, half, 2, dtype=F32) / half)
    ar = r[:, None] * inv
    ac = col[:, None] * inv
    ang = jnp.concatenate([ar, ar, ac, ac], axis=-1)
    cos = jnp.concatenate([jnp.cos(ang), jnp.ones((n_ctx, MLA_ROPE), F32)], axis=0)
    sin = jnp.concatenate([jnp.sin(ang), jnp.zeros((n_ctx, MLA_ROPE), F32)], axis=0)
    return cos, sin


def _rope(x, cos, sin):
    a, b, c, d = jnp.split(x, 4, axis=-1)
    rot = jnp.concatenate([-b, a, -d, c], axis=-1)
    return x * cos + rot * sin


def _pad_cols(w, width):
    return jnp.pad(w, ((0, 0), (0, width - w.shape[1])))


def kernel(x, c, ctx, c_ctx, ada_w, ada_b, norm_mix_pre, norm_mix_post, norm_ffn_pre, norm_ffn_post,
           w_in, mla_q_norm, mla_w_q_up, mla_kv_norm, mla_w_kv_up, gdn_conv, gdn_a_log, gdn_dt_bias,
           gdn_norm, w_out, router_w, router_bias, exp_w_gate, exp_w_up, exp_w_down,
           sh_w_gate, sh_w_up, sh_w_down):
    bsz, n_lat, d = x.shape
    n_ctx = ctx.shape[1]
    lt = n_lat + n_ctx
    depth = ada_w.shape[0]
    q_rank = mla_q_norm.shape[1]
    kv_rank = mla_kv_norm.shape[1]
    n_exp = router_w.shape[2]
    hd = HEADS
    gdn_qkv = hd * (2 * GDN_DK + GDN_DV)
    gdn_w = hd * GDN_DV
    assert q_rank == 2 * kv_rank and MLA_ROPE <= LANE and 2 * hd <= LANE

    c_kpe = q_rank + kv_rank
    c_ba = c_kpe + LANE
    c_qkv = c_ba + LANE
    c_z = c_qkv + gdn_qkv
    assert c_qkv % LANE == 0 and c_z % gdn_w == 0
    s0 = q_rank + kv_rank
    s1 = s0 + MLA_ROPE
    s2 = s1 + gdn_qkv
    s3 = s2 + gdn_w

    xa = jnp.concatenate([x, ctx], axis=1)
    cs = jnp.concatenate([c, c_ctx[None, :]], axis=0)
    cs = jnp.pad(cs, ((0, -(bsz + 1) % 8), (0, 0)))
    mod = _ada_mod(cs, ada_w, ada_b)
    cos, sin = _rope_tables(n_lat, n_ctx)
    nc = lt // GDN_CHUNK
    tb = 256
    t_all = bsz * lt
    n_blocks = -(-t_all * TOP_K // tb) + n_exp

    for i in range(depth):
        m6 = mod[i].reshape(-1, N_MOD, d)
        tabs = [jnp.stack([m6[:bsz, k], jnp.broadcast_to(m6[bsz, k], (bsz, d))], axis=1)
                for k in range(N_MOD)]
        sh1, sc1, g1, sh2, sc2, g2 = tabs

        wi = w_in[i]
        w_in_p = jnp.concatenate(
            [wi[:, :s0], _pad_cols(wi[:, s0:s1], LANE), _pad_cols(wi[:, s3:], LANE), wi[:, s1:s3]],
            axis=1).astype(BF16)
        proj = _norm_matmul(xa, 0, d, norm_mix_pre[i], w_in_p, n_lat, sh1, sc1)

        wq = mla_w_q_up[i].reshape(q_rank, hd, MLA_QK)
        wq = jnp.concatenate([wq[:, :, :MLA_NOPE].reshape(q_rank, -1),
                              wq[:, :, MLA_NOPE:].reshape(q_rank, -1)], axis=1).astype(BF16)
        wkv = mla_w_kv_up[i].reshape(kv_rank, hd, MLA_NOPE + MLA_V)
        wkv = jnp.concatenate([wkv[:, :, :MLA_NOPE].reshape(kv_rank, -1),
                               wkv[:, :, MLA_NOPE:].reshape(kv_rank, -1)], axis=1).astype(BF16)
        qf = _norm_matmul(proj, 0, q_rank, mla_q_norm[i], wq, n_lat)
        kvf = _norm_matmul(proj, q_rank // kv_rank, kv_rank, mla_kv_norm[i], wkv, n_lat)
        q_nope = qf[..., :hd * MLA_NOPE].reshape(bsz, lt, hd, MLA_NOPE)
        q_pe = _rope(qf[..., hd * MLA_NOPE:].reshape(bsz, lt, hd, MLA_ROPE),
                     cos[:, None, :], sin[:, None, :])
        qh = jnp.concatenate([q_nope, q_pe], axis=-1).astype(BF16)
        k_nope = kvf[..., :hd * MLA_NOPE].reshape(bsz, lt, hd, MLA_NOPE)
        k_pe = _rope(proj[..., c_kpe:c_kpe + MLA_ROPE], cos, sin)
        k_pe = jnp.broadcast_to(k_pe[:, :, None, :], (bsz, lt, hd, MLA_ROPE))
        kh = jnp.concatenate([k_nope, k_pe], axis=-1).astype(BF16)
        vh = kvf[..., hd * MLA_NOPE:].reshape(bsz, lt, hd, MLA_V).astype(BF16)
        attn = _attention(qh.transpose(0, 2, 1, 3), kh.transpose(0, 2, 1, 3),
                          vh.transpose(0, 2, 1, 3), n_lat)

        qkv = _gdn_prep(proj, c_qkv // LANE, gdn_conv[i], n_lat)
        beta = jax.nn.sigmoid(proj[..., c_ba:c_ba + 2 * hd]).reshape(bsz, lt, 2, hd)
        alpha = proj[..., c_ba + 2 * hd:c_ba + 4 * hd].reshape(bsz, lt, 2, hd)
        g = -jnp.exp(gdn_a_log[i]) * jax.nn.softplus(alpha + gdn_dt_bias[i])
        to_col = lambda a: a.reshape(bsz, nc, GDN_CHUNK, 2, hd).transpose(0, 3, 1, 2, 4)
        o2 = _gdn_scan(qkv, to_col(beta), to_col(g), to_col(g).swapaxes(-1, -2), n_lat)

        xa, h2 = _out_proj(attn, o2, proj, c_z // gdn_w, gdn_norm[i], w_out[i].astype(BF16), xa,
                           norm_mix_post[i], g1, norm_ffn_pre[i], sh2, sc2, n_lat)

        hflat = h2.reshape(t_all, d)
        idx_t, wts_t, chosen_t = _router(hflat, router_w[i].T, router_bias[i])
        slot_tok, block_e, n_used, tok_slots = _dispatch(idx_t, chosen_t, tb, n_blocks)
        ys = _experts(hflat, slot_tok, block_e, n_used, exp_w_gate[i].astype(BF16),
                      exp_w_up[i].astype(BF16), exp_w_down[i].astype(BF16), tb)
        xa = _combine(tok_slots.reshape(bsz, lt, TOP_K), ys, wts_t.T.reshape(bsz, lt, TOP_K), h2, xa,
                      sh_w_gate[i].astype(BF16), sh_w_up[i].astype(BF16), sh_w_down[i].astype(BF16),
                      norm_ffn_post[i], g2, n_lat)

    return xa[:, :n_lat]
```
